```python
import jax, jax.numpy as jnp
from jax import lax
import numpy as np

D_MODEL = 1024
BATCH = 2
SEQ = 8192
DEPTH = 1

CTX_LEN = 256
GRID_W = 64
HEAD_SIZE = 64
N_HEADS_A = 16
WIDTH_A = N_HEADS_A * HEAD_SIZE
WIDTH_B = D_MODEL
CONV_K = 31
LORA_W = 64
LORA_A = 64
RWKV_COLS = 3 * WIDTH_A + 2 * LORA_W + 2 * LORA_A
IN_COLS = RWKV_COLS + WIDTH_A + 2 * WIDTH_B + WIDTH_B + 2 * D_MODEL
MOD_COLS = 3 * D_MODEL
NORM_EPS = 1e-6
LNX_EPS = 64e-5
LN_EPS = 1e-5

kernel_name = 'bidir_rwkv7_conformer_hybrid_dit'


def rmsnorm(x, w):
    xf = x.astype(jnp.float32)
    y = xf * lax.rsqrt(jnp.mean(xf * xf, axis=-1, keepdims=True) + NORM_EPS)
    return (y * w.astype(jnp.float32)).astype(x.dtype)


def layernorm(x, w, b):
    xf = x.astype(jnp.float32)
    mu = jnp.mean(xf, axis=-1, keepdims=True)
    var = jnp.mean(jnp.square(xf - mu), axis=-1, keepdims=True)
    return ((xf - mu) * lax.rsqrt(var + LN_EPS) * w + b).astype(x.dtype)


def grid_shift(p, rows):
    b, t, ch = p.shape
    q = p.reshape(b, rows, GRID_W, ch // 4, 4)
    left = jnp.pad(q[:, :, :-1, :, 0], ((0, 0), (0, 0), (1, 0), (0, 0)))
    right = jnp.pad(q[:, :, 1:, :, 1], ((0, 0), (0, 0), (0, 1), (0, 0)))
    up = jnp.pad(q[:, :-1, :, :, 2], ((0, 0), (1, 0), (0, 0), (0, 0)))
    down = jnp.pad(q[:, 1:, :, :, 3], ((0, 0), (0, 1), (0, 0), (0, 0)))
    return jnp.stack([left, right, up, down], axis=-1).reshape(b, t, ch)


def seq_shift(p):
    b, t, ch = p.shape
    q = p.reshape(b, t, ch // 2, 2)
    prev = jnp.pad(q[:, :-1, :, 0], ((0, 0), (1, 0), (0, 0)))
    nxt = jnp.pad(q[:, 1:, :, 1], ((0, 0), (0, 1), (0, 0)))
    return jnp.stack([prev, nxt], axis=-1).reshape(b, t, ch)


def rwkv_prepare(mixed, w0, w2, a0, a2, k_k, k_a):
    b, t, _ = mixed.shape
    m = mixed.astype(jnp.float32)
    o1, o2, o3 = WIDTH_A, 2 * WIDTH_A, 3 * WIDTH_A
    r, k, v = m[..., :o1], m[..., o1:o2], m[..., o2:o3]
    low = m[..., o3:]
    wd = (low[..., :LORA_W], low[..., LORA_W:2 * LORA_W])
    ad = (low[..., 2 * LORA_W:2 * LORA_W + LORA_A], low[..., 2 * LORA_W + LORA_A:])
    heads = lambda z: z.reshape(b, t, N_HEADS_A, HEAD_SIZE)
    kk = heads(k * k_k)
    kk = kk * lax.rsqrt(jnp.maximum(jnp.sum(kk * kk, axis=-1, keepdims=True), 1e-24))
    dirs = []
    for d in range(2):
        wlog = -jax.nn.softplus(-(w0[d] + jnp.tanh(wd[d]) @ w2[d])) - 0.5
        decay = jnp.exp(-jnp.exp(wlog))
        a = jax.nn.sigmoid(a0[d] + ad[d] @ a2[d])
        kd = k * (1.0 + (a - 1.0) * k_a)
        dirs.append((heads(decay), heads(kd), kk * heads(a)))
    return heads(r), heads(v), kk, dirs


def wkv_scan(s0, r, w, k, v, a_vec, b_vec, reverse, inclusive):
    def step(s, inp):
        rt, wt, kt, vt, at, bt = inp
        sa = jnp.einsum('bhij,bhj->bhi', s, at)
        s_new = s * wt[:, :, None, :] + sa[..., None] * bt[:, :, None, :] + vt[..., None] * kt[:, :, None, :]
        y = jnp.einsum('bhij,bhj->bhi', s_new if inclusive else s, rt)
        return s_new, y
    xs = tuple(jnp.moveaxis(z, 1, 0) for z in (r, w, k, v, a_vec, b_vec))
    s_fin, ys = lax.scan(step, s0, xs, reverse=reverse)
    return jnp.moveaxis(ys, 0, 1), s_fin


def rwkv_scans(prep, s0_f, s0_b):
    r, v, kk, dirs = prep
    (dec_f, k_f, b_f), (dec_b, k_b, b_b) = dirs
    y_f, s_f = wkv_scan(s0_f, r, dec_f, k_f, v, -kk, b_f, False, True)
    y_b, s_b = wkv_scan(s0_b, r, dec_b, k_b, v, -kk, b_b, True, False)
    return y_f + y_b, s_f, s_b


def rwkv_readout(y_sum, prep, r_k, lnx_w, lnx_b):
    r, v, kk, dirs = prep
    k_f = dirs[0][1]
    b, t = y_sum.shape[:2]
    mu = jnp.mean(y_sum, axis=-1, keepdims=True)
    var = jnp.mean(jnp.square(y_sum - mu), axis=-1, keepdims=True)
    yn = ((y_sum - mu) * lax.rsqrt(var + LNX_EPS)).reshape(b, t, WIDTH_A) * lnx_w + lnx_b
    bonus = jnp.sum(r * k_f * r_k, axis=-1, keepdims=True) * v
    return yn + bonus.reshape(b, t, WIDTH_A)


def conformer_conv(glu_in, conv_w, conv_b, cln_w, cln_b):
    u, g = jnp.split(glu_in, 2, axis=-1)
    z = u * jax.nn.sigmoid(g)
    z = lax.conv_general_dilated(z, conv_w[:, None, :], window_strides=(1,),
                                 padding=[(CONV_K // 2, CONV_K // 2)],
                                 dimension_numbers=('NWC', 'WIO', 'NWC'),
                                 feature_group_count=WIDTH_B) + conv_b
    return jax.nn.silu(layernorm(z, cln_w, cln_b))


def mixer_out(p, y_sum, prep, r_k, lnx_w, lnx_b, conv_w, conv_b, cln_w, cln_b, w_proj_a, w_proj_b, w_out):
    o = RWKV_COLS
    g_a = p[..., o:o + WIDTH_A]; o += WIDTH_A
    glu = p[..., o:o + 2 * WIDTH_B]; o += 2 * WIDTH_B
    g_b = p[..., o:o + WIDTH_B]; o += WIDTH_B
    m_a = p[..., o:o + D_MODEL]
    m_b = p[..., o + D_MODEL:]
    y_a = rwkv_readout(y_sum, prep, r_k, lnx_w, lnx_b).astype(p.dtype) * jax.nn.silu(g_a)
    y_b = conformer_conv(glu, conv_w, conv_b, cln_w, cln_b) * jax.nn.silu(g_b)
    merged = jax.nn.sigmoid(m_a) * (y_a @ w_proj_a) + jax.nn.sigmoid(m_b) * (y_b @ w_proj_b)
    return merged @ w_out


def hybrid_layer(x, xc, c, c_ctx, rows, last, norm_w, w_mod, b_mod, w_in, mu_shift, w0, w2, a0, a2,
                 k_k, k_a, r_k, lnx_w, lnx_b, conv_w, conv_b, cln_w, cln_b, w_proj_a, w_proj_b, w_out):
    shift, scale, gate = jnp.split(jax.nn.silu(c) @ w_mod + b_mod, 3, axis=-1)
    shift_c, scale_c, gate_c = jnp.split(jax.nn.silu(c_ctx) @ w_mod + b_mod, 3, axis=-1)
    h = rmsnorm(x, norm_w) * (1.0 + scale[:, None]) + shift[:, None]
    hc = rmsnorm(xc, norm_w) * (1.0 + scale_c) + shift_c
    p = h @ w_in
    pc = hc @ (w_in[:, :RWKV_COLS] if last else w_in)
    rk = p[..., :RWKV_COLS]
    rk = rk + mu_shift * (grid_shift(rk, rows) - rk)
    rkc = pc[..., :RWKV_COLS]
    rkc = rkc + mu_shift * (seq_shift(rkc) - rkc)
    prep_c = rwkv_prepare(rkc, w0, w2, a0, a2, k_k, k_a)
    prep = rwkv_prepare(rk, w0, w2, a0, a2, k_k, k_a)
    s0 = jnp.zeros((x.shape[0], N_HEADS_A, HEAD_SIZE, HEAD_SIZE), jnp.float32)
    yc_sum, sc_f, sc_b = rwkv_scans(prep_c, s0, s0)
    y_sum, _, _ = rwkv_scans(prep, sc_f, sc_b)
    x_new = x + gate[:, None] * mixer_out(p, y_sum, prep, r_k, lnx_w, lnx_b, conv_w, conv_b,
                                          cln_w, cln_b, w_proj_a, w_proj_b, w_out)
    if last:
        return x_new, xc
    xc_new = xc + gate_c * mixer_out(pc, yc_sum, prep_c, r_k, lnx_w, lnx_b, conv_w, conv_b,
                                     cln_w, cln_b, w_proj_a, w_proj_b, w_out)
    return x_new, xc_new


def setup_inputs(seed: int = 0) -> dict:
    key = jax.random.key(seed)
    ks = jax.random.split(key, 32)
    L, D = DEPTH, D_MODEL
    f32 = jnp.float32
    nrm = lambda k, shape, s: jax.random.normal(k, shape, f32) * s
    return {
        'x': nrm(ks[0], (BATCH, SEQ, D), 1.0),
        'c': nrm(ks[1], (BATCH, D), 1.0),
        'ctx': nrm(ks[2], (BATCH, CTX_LEN, D), 1.0),
        'c_ctx': nrm(ks[3], (D,), 1.0),
        'norm_w': 1.0 + nrm(ks[4], (L, D), 0.02),
        'w_mod': nrm(ks[5], (L, D, MOD_COLS), 0.5 * D ** -0.5),
        'b_mod': nrm(ks[6], (L, MOD_COLS), 0.02),
        'w_in': nrm(ks[7], (L, D, IN_COLS), D ** -0.5),
        'mu_shift': jax.random.uniform(ks[8], (L, RWKV_COLS), f32, 0.0, 1.0),
        'w0': jax.random.uniform(ks[9], (L, 2, WIDTH_A), f32, -6.0, 1.0),
        'w2': nrm(ks[10], (L, 2, LORA_W, WIDTH_A), 0.5 * LORA_W ** -0.5),
        'a0': nrm(ks[11], (L, 2, WIDTH_A), 0.5),
        'a2': nrm(ks[12], (L, 2, LORA_A, WIDTH_A), 0.5 * LORA_A ** -0.5),
        'k_k': 0.85 + nrm(ks[13], (L, WIDTH_A), 0.05),
        'k_a': 1.0 + nrm(ks[14], (L, WIDTH_A), 0.05),
        'r_k': nrm(ks[15], (L, N_HEADS_A, HEAD_SIZE), 0.1),
        'lnx_w': 1.0 + nrm(ks[16], (L, WIDTH_A), 0.02),
        'lnx_b': nrm(ks[17], (L, WIDTH_A), 0.02),
        'conv_w': nrm(ks[18], (L, CONV_K, WIDTH_B), CONV_K ** -0.5),
        'conv_b': nrm(ks[19], (L, WIDTH_B), 0.02),
        'cln_w': 1.0 + nrm(ks[20], (L, WIDTH_B), 0.02),
        'cln_b': nrm(ks[21], (L, WIDTH_B), 0.02),
        'w_proj_a': nrm(ks[22], (L, WIDTH_A, D), WIDTH_A ** -0.5),
        'w_proj_b': nrm(ks[23], (L, WIDTH_B, D), WIDTH_B ** -0.5),
        'w_out': nrm(ks[24], (L, D, D), D ** -0.5),
        'final_norm_w': 1.0 + nrm(ks[25], (D,), 0.02),
    }


def reference(x, c, ctx, c_ctx, norm_w, w_mod, b_mod, w_in, mu_shift, w0, w2, a0, a2, k_k, k_a, r_k,
              lnx_w, lnx_b, conv_w, conv_b, cln_w, cln_b, w_proj_a, w_proj_b, w_out, final_norm_w):
    rows = x.shape[1] // GRID_W
    xc = ctx
    for l in range(DEPTH):
        x, xc = hybrid_layer(x, xc, c, c_ctx, rows, l == DEPTH - 1, norm_w[l], w_mod[l], b_mod[l],
                             w_in[l], mu_shift[l], w0[l], w2[l], a0[l], a2[l], k_k[l], k_a[l], r_k[l],
                             lnx_w[l], lnx_b[l], conv_w[l], conv_b[l], cln_w[l], cln_b[l],
                             w_proj_a[l], w_proj_b[l], w_out[l])
    return rmsnorm(x, final_norm_w)
```

```python
import functools

import jax
import jax.numpy as jnp
from jax import lax
from jax.experimental import pallas as pl
from jax.experimental.pallas import tpu as pltpu

F32 = jnp.float32
BF16 = jnp.bfloat16

HEAD = 64
CHUNK = 64
GRID_W = 64
CONV_K = 31
LANES = 128
NORM_EPS = 1e-6
LNX_EPS = 64e-5
LN_EPS = 1e-5
VMEM_LIMIT_BYTES = 56 * 1024 * 1024

_HI = lax.Precision.HIGHEST


def _sigmoid(x):
    return 1.0 / (1.0 + jnp.exp(-x))


def _silu(x):
    return x * _sigmoid(x)


def _split2(x):
    hi = x.astype(BF16)
    lo = (x - hi.astype(F32)).astype(BF16)
    return hi, lo


def _split3(x):
    hi = x.astype(BF16)
    r1 = x - hi.astype(F32)
    mid = r1.astype(BF16)
    lo = (r1 - mid.astype(F32)).astype(BF16)
    return hi, mid, lo


def _seg_sum(z, e_bf16):
    hi, lo = _split2(z)
    return (jnp.dot(hi, e_bf16, preferred_element_type=F32)
            + jnp.dot(lo, e_bf16, preferred_element_type=F32))


def _tri_dot(m_bf16, x):
    hi, mid, lo = _split3(x)
    return (jnp.dot(m_bf16, hi, preferred_element_type=F32)
            + jnp.dot(m_bf16, mid, preferred_element_type=F32)
            + jnp.dot(m_bf16, lo, preferred_element_type=F32))


def _mod_kernel(c_ref, w_ref, b_ref, o_ref):
    o_ref[...] = jnp.dot(_silu(c_ref[...]), w_ref[...], precision=_HI,
                         preferred_element_type=F32) + b_ref[...]


def _modulation(cc, w_mod, b_mod):
    rows, d = cc.shape
    n = w_mod.shape[1]
    tn = d
    return pl.pallas_call(
        _mod_kernel,
        grid=(n // tn,),
        in_specs=[pl.BlockSpec((rows, d), lambda j: (0, 0)),
                  pl.BlockSpec((d, tn), lambda j: (0, j)),
                  pl.BlockSpec((1, tn), lambda j: (0, j))],
        out_specs=pl.BlockSpec((rows, tn), lambda j: (0, j)),
        out_shape=jax.ShapeDtypeStruct((rows, n), F32),
        compiler_params=pltpu.CompilerParams(vmem_limit_bytes=VMEM_LIMIT_BYTES),
    )(cc, w_mod, b_mod)


def _norm_mod_kernel(x_ref, nw_ref, shift_ref, scale_ref, h_ref):
    x = x_ref[...]
    ms = jnp.mean(x * x, axis=-1, keepdims=True)
    y = x * lax.rsqrt(ms + NORM_EPS) * nw_ref[...]
    h_ref[...] = (y * (1.0 + scale_ref[...]) + shift_ref[...]).astype(h_ref.dtype)


def _norm_mod(x2d, norm_w, shift3, scale3, mod_row_of_tile, tm):
    rows, d = x2d.shape
    return pl.pallas_call(
        _norm_mod_kernel,
        grid=(rows // tm,),
        in_specs=[pl.BlockSpec((tm, d), lambda i: (i, 0)),
                  pl.BlockSpec((1, d), lambda i: (0, 0)),
                  pl.BlockSpec((None, 1, d), lambda i: (mod_row_of_tile(i), 0, 0)),
                  pl.BlockSpec((None, 1, d), lambda i: (mod_row_of_tile(i), 0, 0))],
        out_specs=pl.BlockSpec((tm, d), lambda i: (i, 0)),
        out_shape=jax.ShapeDtypeStruct((rows, d), BF16),
        compiler_params=pltpu.CompilerParams(vmem_limit_bytes=VMEM_LIMIT_BYTES),
    )(x2d, norm_w, shift3, scale3)


def _matmul_kernel(h_ref, w_ref, o_ref):
    o_ref[...] = jnp.dot(h_ref[...], w_ref[...], preferred_element_type=F32).astype(o_ref.dtype)


def _matmul(h, w, tm, tn):
    rows, d = h.shape
    n = w.shape[1]
    return pl.pallas_call(
        _matmul_kernel,
        grid=(n // tn, rows // tm),
        in_specs=[pl.BlockSpec((tm, d), lambda j, i: (i, 0)),
                  pl.BlockSpec((d, tn), lambda j, i: (0, j))],
        out_specs=pl.BlockSpec((tm, tn), lambda j, i: (i, j)),
        out_shape=jax.ShapeDtypeStruct((rows, n), BF16),
        compiler_params=pltpu.CompilerParams(vmem_limit_bytes=VMEM_LIMIT_BYTES),
    )(h, w)


def _prep_kernel(cur_ref, prev_ref, next_ref, mu_ref, w0_ref, w2_ref, a0_ref, a2_ref,
                 kk_ref, ka_ref, rk_ref, e_ref,
                 v_o, atf_o, btf_o, ktf_o, rtf_o, bhf_o, khf_o, wtf_o,
                 atb_o, btb_o, ktb_o, rtb_o, bhb_o, khb_o, wtb_o, bonus_o,
                 *, grid_mode, tm, n_tiles, width):
    i = pl.program_id(1)
    x = cur_ref[0].astype(F32)
    row = lax.broadcasted_iota(jnp.int32, x.shape, 0)
    lane = lax.broadcasted_iota(jnp.int32, x.shape, 1)
    back1 = pltpu.roll(x, 1, 0)
    fwd1 = pltpu.roll(x, tm - 1, 0)
    if grid_mode:
        col = row & (GRID_W - 1)
        left = jnp.where(col == 0, 0.0, back1)
        right = jnp.where(col == GRID_W - 1, 0.0, fwd1)
        prev_h = prev_ref[0].astype(F32) * (i > 0).astype(F32)
        next_h = next_ref[0].astype(F32) * (i < n_tiles - 1).astype(F32)
        up = jnp.concatenate([prev_h, x[:tm - GRID_W]], axis=0)
        down = jnp.concatenate([x[GRID_W:], next_h], axis=0)
        l4 = lane & 3
        shifted = jnp.where(l4 == 0, left, jnp.where(l4 == 1, right, jnp.where(l4 == 2, up, down)))
    else:
        prev_t = jnp.where(row == 0, 0.0, back1)
        next_t = jnp.where(row == tm - 1, 0.0, fwd1)
        shifted = jnp.where((lane & 1) == 0, prev_t, next_t)
    rkm = x + mu_ref[...] * (shifted - x)

    w = width
    r = rkm[:, :w]
    k = rkm[:, w:2 * w]
    v = rkm[:, 2 * w:3 * w]
    low = rkm[:, 3 * w:]
    lora = (low.shape[1]) // 4
    e = e_ref[...]

    kkr = k * kk_ref[...]
    ss = _seg_sum(kkr * kkr, e)
    kkn = kkr * lax.rsqrt(jnp.maximum(ss, 1e-24))

    tr = lax.broadcasted_iota(jnp.int32, (tm, tm), 0)
    tc = lax.broadcasted_iota(jnp.int32, (tm, tm), 1)
    same = (tr // CHUNK) == (tc // CHUNK)
    ind = lambda cond: jnp.where(same & cond, 1.0, 0.0).astype(BF16)
    m_le = ind(tc <= tr)
    m_gt = ind(tc > tr)
    m_ge = ind(tc >= tr)
    m_lt = ind(tc < tr)

    outs = ((atf_o, btf_o, ktf_o, rtf_o, bhf_o, khf_o, wtf_o),
            (atb_o, btb_o, ktb_o, rtb_o, bhb_o, khb_o, wtb_o))
    kd_fwd = None
    for d in range(2):
        wd = low[:, d * lora:(d + 1) * lora]
        ad = low[:, (2 + d) * lora:(3 + d) * lora]
        z = w0_ref[d:d + 1, :] + jnp.dot(jnp.tanh(wd), w2_ref[d], precision=_HI,
                                         preferred_element_type=F32)
        softplus_neg = jnp.maximum(-z, 0.0) + jnp.log(1.0 + jnp.exp(-jnp.abs(z)))
        ld = -jnp.exp(-softplus_neg - 0.5)
        a = _sigmoid(a0_ref[d:d + 1, :] + jnp.dot(ad, a2_ref[d], precision=_HI,
                                                  preferred_element_type=F32))
        kd = k * (1.0 + (a - 1.0) * ka_ref[...])
        bv = kkn * a
        if d == 0:
            kd_fwd = kd
            cum = _tri_dot(m_le, ld)
            rest = _tri_dot(m_gt, ld)
            cum_r = cum
        else:
            cum = _tri_dot(m_ge, ld)
            rest = _tri_dot(m_lt, ld)
            cum_r = cum - ld
        e_neg = jnp.exp(-cum)
        e_rest = jnp.exp(rest)
        at_o, bt_o, kt_o, rt_o, bh_o, kh_o, wt_o = outs[d]
        at_o[0] = (-kkn * jnp.exp(cum - ld)).astype(at_o.dtype)
        bt_o[0] = (bv * e_neg).astype(bt_o.dtype)
        kt_o[0] = (kd * e_neg).astype(kt_o.dtype)
        rt_o[0] = (r * jnp.exp(cum_r)).astype(rt_o.dtype)
        bh_o[0] = (bv * e_rest).astype(bh_o.dtype)
        kh_o[0] = (kd * e_rest).astype(kh_o.dtype)
        tot = cum + rest
        wt_o[0] = jnp.exp(jnp.concatenate(
            [tot[c * CHUNK:c * CHUNK + 8] for c in range(tm // CHUNK)], axis=0))

    v_o[0] = v.astype(v_o.dtype)
    bonus = _seg_sum(r * kd_fwd * rk_ref[...], e) * v
    bonus_o[0] = bonus.astype(bonus_o.dtype)


def _prepare(rk3, mu, w0, w2, a0, a2, k_k, k_a, r_k, e_mat, grid_mode, tm):
    b, t, cols = rk3.shape
    width = k_k.shape[1]
    n_tiles = t // tm
    hb = tm // GRID_W
    n_hblk = t // GRID_W
    full = lambda shape: pl.BlockSpec(shape, lambda bi, i: (0,) * len(shape))
    big = pl.BlockSpec((1, tm, width), lambda bi, i: (bi, i, 0))
    wt = pl.BlockSpec((1, tm // 8, width), lambda bi, i: (bi, i, 0))
    big_shape = jax.ShapeDtypeStruct((b, t, width), BF16)
    wt_shape = jax.ShapeDtypeStruct((b, t // 8, width), F32)
    dir_specs = [big] * 6 + [wt]
    dir_shapes = [big_shape] * 6 + [wt_shape]
    kern = functools.partial(_prep_kernel, grid_mode=grid_mode, tm=tm, n_tiles=n_tiles, width=width)
    return pl.pallas_call(
        kern,
        grid=(b, n_tiles),
        in_specs=[pl.BlockSpec((1, tm, cols), lambda bi, i: (bi, i, 0)),
                  pl.BlockSpec((1, GRID_W, cols), lambda bi, i: (bi, jnp.maximum(i * hb - 1, 0), 0)),
                  pl.BlockSpec((1, GRID_W, cols), lambda bi, i: (bi, jnp.minimum((i + 1) * hb, n_hblk - 1), 0)),
                  full(mu.shape), full(w0.shape), full(w2.shape), full(a0.shape), full(a2.shape),
                  full(k_k.shape), full(k_a.shape), full(r_k.shape), full(e_mat.shape)],
        out_specs=[big] + dir_specs + dir_specs + [big],
        out_shape=[big_shape] + dir_shapes + dir_shapes + [big_shape],
        compiler_params=pltpu.CompilerParams(vmem_limit_bytes=VMEM_LIMIT_BYTES),
    )(rk3, rk3, rk3, mu, w0, w2, a0, a2, k_k, k_a, r_k, e_mat)


def _mm(x, y):
    return jnp.dot(x.astype(BF16), y.astype(BF16), preferred_element_type=F32)


def _mm_nt(x, y):
    return lax.dot_general(x.astype(BF16), y.astype(BF16), (((1,), (1,)), ((), ())),
                           preferred_element_type=F32)


def _mm_tn(x, y):
    return jnp.dot(jnp.transpose(x.astype(F32)).astype(BF16), y.astype(BF16),
                   preferred_element_type=F32)


def _block_diag(x2):
    lane = lax.broadcasted_iota(jnp.int32, x2.shape, 1)
    z = jnp.zeros_like(x2)
    return jnp.concatenate([jnp.where(lane < HEAD, x2, z), jnp.where(lane >= HEAD, x2, z)], axis=0)


def _diag_blocks(g):
    lane = lax.broadcasted_iota(jnp.int32, (HEAD, LANES), 1)
    return jnp.where(lane < HEAD, g[:HEAD], g[HEAD:])


def _chunk_step(at, bt, kt, rt, bh, kh, v, wrow, st, backward):
    row = lax.broadcasted_iota(jnp.int32, (CHUNK, LANES), 0)
    lane = lax.broadcasted_iota(jnp.int32, (CHUNK, LANES), 1)
    s = lane & (HEAD - 1)
    if backward:
        mask_u = s > row
        mask_y = mask_u
    else:
        mask_u = s < row
        mask_y = s <= row
    eye2 = jnp.where(s == row, 1.0, 0.0).astype(F32)
    ar = jnp.concatenate([at, rt], axis=0)
    gb = _mm_nt(ar, _block_diag(bt))
    gk = _mm_nt(ar, _block_diag(kt))
    l1 = jnp.where(mask_u, gb[:CHUNK], 0.0)
    aak = jnp.where(mask_u, gk[:CHUNK], 0.0)
    arb = jnp.where(mask_y, gb[CHUNK:], 0.0)
    ark = jnp.where(mask_y, gk[CHUNK:], 0.0)
    t = eye2 + l1
    lp = _mm(l1, _block_diag(l1))
    for _ in range(4):
        res = _mm(jnp.concatenate([t, lp], axis=0), _block_diag(lp))
        t = t + res[:CHUNK]
        lp = res[CHUNK:]
    t = t + _mm(t, _block_diag(lp))
    bdv = _block_diag(v)
    wv = _mm(aak, bdv)
    pq = _mm(t, jnp.concatenate([_block_diag(at), _block_diag(wv)], axis=1))
    p = pq[:, :LANES]
    q = pq[:, LANES:]
    ryq = _mm(arb, jnp.concatenate([_block_diag(p), _block_diag(q)], axis=1))
    ry = rt.astype(F32) + ryq[:, :LANES]
    y0 = ryq[:, LANES:] + _mm(ark, bdv)
    mz = _mm_tn(bh, jnp.concatenate([p, q], axis=1))
    kv = _mm_tn(kh, v)
    mt = _diag_blocks(mz[:, :LANES])
    zt = _diag_blocks(mz[:, LANES:] + kv)
    wcol = _diag_blocks(jnp.transpose(jnp.broadcast_to(wrow, (LANES, LANES))))
    ys = _mm(jnp.concatenate([ry, mt], axis=0), _block_diag(st))
    y = ys[:CHUNK] + y0
    st_new = wcol * st + ys[CHUNK:] + zt
    return y, st_new


def _scan_kernel(vf, atf, btf, ktf, rtf, bhf, khf, wtf,
                 vb, atb, btb, ktb, rtb, bhb, khb, wtb, s0_ref,
                 yf_o, yb_o, sfin_o, st_scr, *, nch, n_tiles):
    i = pl.program_id(2)

    @pl.when(i == 0)
    def _():
        st_scr[...] = s0_ref[0, 0]

    for c in range(nch):
        sl = pl.ds(c * CHUNK, CHUNK)
        y, st = _chunk_step(atf[0, sl, :], btf[0, sl, :], ktf[0, sl, :], rtf[0, sl, :],
                            bhf[0, sl, :], khf[0, sl, :], vf[0, sl, :],
                            wtf[0, c * 8:c * 8 + 1, :], st_scr[0], False)
        yf_o[0, sl, :] = y
        st_scr[0] = st
        cb = nch - 1 - c
        sb = pl.ds(cb * CHUNK, CHUNK)
        y, st = _chunk_step(atb[0, sb, :], btb[0, sb, :], ktb[0, sb, :], rtb[0, sb, :],
                            bhb[0, sb, :], khb[0, sb, :], vb[0, sb, :],
                            wtb[0, cb * 8:cb * 8 + 1, :], st_scr[1], True)
        yb_o[0, sb, :] = y
        st_scr[1] = st

    @pl.when(i == n_tiles - 1)
    def _():
        sfin_o[0, 0] = st_scr[...]


def _scan(prep, s0, tm):
    v, f6, wtf, b6, wtb = prep
    b, t, width = v.shape
    n_pairs = width // LANES
    n_tiles = t // tm
    nch = tm // CHUNK
    fwd = pl.BlockSpec((1, tm, LANES), lambda bi, p, i: (bi, i, p))
    bwd = pl.BlockSpec((1, tm, LANES), lambda bi, p, i: (bi, n_tiles - 1 - i, p))
    fwd_w = pl.BlockSpec((1, tm // 8, LANES), lambda bi, p, i: (bi, i, p))
    bwd_w = pl.BlockSpec((1, tm // 8, LANES), lambda bi, p, i: (bi, n_tiles - 1 - i, p))
    st_spec = pl.BlockSpec((1, 1, 2, HEAD, LANES), lambda bi, p, i: (bi, p, 0, 0, 0))
    kern = functools.partial(_scan_kernel, nch=nch, n_tiles=n_tiles)
    return pl.pallas_call(
        kern,
        grid=(b, n_pairs, n_tiles),
        in_specs=[fwd] * 7 + [fwd_w] + [bwd] * 7 + [bwd_w] + [st_spec],
        out_specs=[fwd, bwd, st_spec],
        out_shape=[jax.ShapeDtypeStruct((b, t, width), F32),
                   jax.ShapeDtypeStruct((b, t, width), F32),
                   jax.ShapeDtypeStruct(s0.shape, F32)],
        scratch_shapes=[pltpu.VMEM((2, HEAD, LANES), F32)],
        compiler_params=pltpu.CompilerParams(
            dimension_semantics=("arbitrary", "arbitrary", "arbitrary"),
            vmem_limit_bytes=VMEM_LIMIT_BYTES),
    )(v, *f6, wtf, v, *b6, wtb, s0)


def _out_kernel(yf_ref, yb_ref, bonus_ref, ga_ref, u_ref, g_ref, up_ref, gp_ref, un_ref, gn_ref,
                gb_ref, ma_ref, mb_ref, x_ref, gate_ref,
                lnxw_ref, lnxb_ref, cw_ref, cb_ref, clnw_ref, clnb_ref,
                wpa_ref, wpb_ref, wo_ref, fnw_ref, e_ref,
                o_ref, z_scr, *, tm, n_tiles, halo):
    i = pl.program_id(1)
    e = e_ref[...]
    ysum = yf_ref[0] + yb_ref[0]
    mu = _seg_sum(ysum, e) * (1.0 / HEAD)
    dlt = ysum - mu
    var = _seg_sum(dlt * dlt, e) * (1.0 / HEAD)
    yn = dlt * lax.rsqrt(var + LNX_EPS) * lnxw_ref[...] + lnxb_ref[...]
    y_a = (yn + bonus_ref[0].astype(F32)) * _silu(ga_ref[0].astype(F32))

    glu = lambda u, g: u.astype(F32) * _sigmoid(g.astype(F32))
    z_scr[0:halo] = glu(up_ref[0], gp_ref[0]) * (i > 0).astype(F32)
    z_scr[halo:halo + tm] = glu(u_ref[0], g_ref[0])
    z_scr[halo + tm:halo + tm + halo] = glu(un_ref[0], gn_ref[0]) * (i < n_tiles - 1).astype(F32)
    base = halo - CONV_K // 2
    acc = jnp.zeros((tm, z_scr.shape[1]), F32) + cb_ref[...]
    for j in range(CONV_K):
        acc = acc + cw_ref[j:j + 1, :] * z_scr[base + j:base + j + tm, :]
    m1 = jnp.mean(acc, axis=-1, keepdims=True)
    dc = acc - m1
    v1 = jnp.mean(dc * dc, axis=-1, keepdims=True)
    ln = dc * lax.rsqrt(v1 + LN_EPS) * clnw_ref[...] + clnb_ref[...]
    y_b = _silu(ln) * _silu(gb_ref[0].astype(F32))

    pa = jnp.dot(y_a.astype(BF16), wpa_ref[...], preferred_element_type=F32)
    pb = jnp.dot(y_b.astype(BF16), wpb_ref[...], preferred_element_type=F32)
    merged = _sigmoid(ma_ref[0].astype(F32)) * pa + _sigmoid(mb_ref[0].astype(F32)) * pb
    out = jnp.dot(merged.astype(BF16), wo_ref[...], preferred_element_type=F32)
    x_new = x_ref[0] + gate_ref[...] * out
    ms = jnp.mean(x_new * x_new, axis=-1, keepdims=True)
    o_ref[0] = x_new * lax.rsqrt(ms + NORM_EPS) * fnw_ref[...]


def _readout(yf, yb, bonus, prest, x, gate3, lnx_w, lnx_b, conv_w, conv_b, cln_w, cln_b,
             wpa, wpb, wo, fnw, e_mat, tm):
    b, t, d = x.shape
    n_tiles = t // tm
    halo = 16
    hb = tm // halo
    n_hblk = t // halo
    tile = lambda cb: pl.BlockSpec((1, tm, d), lambda bi, i: (bi, i, cb))
    prev = lambda cb: pl.BlockSpec((1, halo, d), lambda bi, i: (bi, jnp.maximum(i * hb - 1, 0), cb))
    nxt = lambda cb: pl.BlockSpec((1, halo, d), lambda bi, i: (bi, jnp.minimum((i + 1) * hb, n_hblk - 1), cb))
    full = lambda shape: pl.BlockSpec(shape, lambda bi, i: (0,) * len(shape))
    kern = functools.partial(_out_kernel, tm=tm, n_tiles=n_tiles, halo=halo)
    return pl.pallas_call(
        kern,
        grid=(b, n_tiles),
        in_specs=[tile(0), tile(0), tile(0),
                  tile(0), tile(1), tile(2), prev(1), prev(2), nxt(1), nxt(2),
                  tile(3), tile(4), tile(5), tile(0),
                  pl.BlockSpec((None, 1, d), lambda bi, i: (bi, 0, 0)),
                  full(lnx_w.shape), full(lnx_b.shape), full(conv_w.shape), full(conv_b.shape),
                  full(cln_w.shape), full(cln_b.shape),
                  full(wpa.shape), full(wpb.shape), full(wo.shape), full(fnw.shape), full(e_mat.shape)],
        out_specs=tile(0),
        out_shape=jax.ShapeDtypeStruct((b, t, d), F32),
        scratch_shapes=[pltpu.VMEM((tm + 2 * halo, d), F32)],
        compiler_params=pltpu.CompilerParams(vmem_limit_bytes=VMEM_LIMIT_BYTES),
    )(yf, yb, bonus, prest, prest, prest, prest, prest, prest, prest, prest, prest, prest, x, gate3,
      lnx_w, lnx_b, conv_w, conv_b, cln_w, cln_b, wpa, wpb, wo, fnw, e_mat)


def kernel(x, c, ctx, c_ctx, norm_w, w_mod, b_mod, w_in, mu_shift, w0, w2, a0, a2, k_k, k_a, r_k,
           lnx_w, lnx_b, conv_w, conv_b, cln_w, cln_b, w_proj_a, w_proj_b, w_out, final_norm_w):
    assert norm_w.shape[0] == 1, "single-layer stack"
    b, s, d = x.shape
    t_ctx = ctx.shape[1]
    width = k_k.shape[1]
    rwkv_cols = mu_shift.shape[1]
    assert width == d and s % 512 == 0 and t_ctx % 256 == 0 and s // GRID_W * GRID_W == s
    row2 = lambda z: z.reshape(1, -1)

    cc = jnp.zeros((8, d), F32).at[:b].set(c).at[b].set(c_ctx)
    mods = _modulation(cc, w_mod[0], row2(b_mod[0]))
    shift3 = mods[:, :d].reshape(8, 1, d)
    scale3 = mods[:, d:2 * d].reshape(8, 1, d)
    gate3 = mods[:, 2 * d:].reshape(8, 1, d)

    w_in_bf = w_in[0].astype(BF16)
    w_rk = w_in_bf[:, :rwkv_cols]
    w_rest = w_in_bf[:, rwkv_cols:]
    tm = 512
    tiles_per_batch = s // tm
    h_lat = _norm_mod(x.reshape(b * s, d), row2(norm_w[0]), shift3, scale3,
                      lambda i: i // tiles_per_batch, tm)
    h_ctx = _norm_mod(ctx.reshape(b * t_ctx, d), row2(norm_w[0]), shift3, scale3,
                      lambda i: b, t_ctx)
    rk_lat = _matmul(h_lat, w_rk, tm, rwkv_cols // 2).reshape(b, s, rwkv_cols)
    rest_lat = _matmul(h_lat, w_rest, tm, 2048).reshape(b, s, -1)
    rk_ctx = _matmul(h_ctx, w_rk, t_ctx, rwkv_cols // 2).reshape(b, t_ctx, rwkv_cols)

    hid = jnp.arange(width, dtype=jnp.int32) // HEAD
    e_mat = (hid[:, None] == hid[None, :]).astype(BF16)
    prep_args = (row2(mu_shift[0]), w0[0], w2[0], a0[0], a2[0], row2(k_k[0]), row2(k_a[0]),
                 r_k[0].reshape(1, -1), e_mat)
    split = lambda o: (o[0], o[1:7], o[7], o[8:14], o[14])
    out_ctx = _prepare(rk_ctx, *prep_args, grid_mode=False, tm=t_ctx)
    out_lat = _prepare(rk_lat, *prep_args, grid_mode=True, tm=256)

    s_zero = jnp.zeros((b, width // LANES, 2, HEAD, LANES), F32)
    _, _, s_ctx = _scan(split(out_ctx), s_zero, t_ctx)
    y_f, y_b, _ = _scan(split(out_lat), s_ctx, 256)

    cw = jnp.zeros((32, d), F32).at[:CONV_K].set(conv_w[0])
    return _readout(y_f, y_b, out_lat[15], rest_lat, x, gate3,
                    row2(lnx_w[0]), row2(lnx_b[0]), cw, row2(conv_b[0]), row2(cln_w[0]), row2(cln_b[0]),
                    w_proj_a[0].astype(BF16), w_proj_b[0].astype(BF16), w_out[0].astype(BF16),
                    row2(final_norm_w), e_mat, 128)
```
